```python
import jax, jax.numpy as jnp
from jax import lax
import numpy as np

D_MODEL = 1024
BATCH = 4
SEQ = 4096
DEPTH = 4
DEC_BATCH = 8
DEC_SEQ = 8192
PAST_LEN = 128

N_HEADS = 8
N_KV_HEADS = 2
HEAD_DIM = 64
GROUP = N_HEADS // N_KV_HEADS
ROT_DIM = HEAD_DIM // 4
ROPE_THETA = 500000.0
WINDOW = 128
BLOCK = 128
CONV_CH = 512
CONV_K = 31
D_FF = 2816
FFN_K = 3
EPS = 1e-6
NEG = -1e30
Q_W = N_HEADS * HEAD_DIM
KV_W = N_KV_HEADS * HEAD_DIM
IN_W = Q_W + 2 * KV_W + 2 * CONV_CH + 2 * D_MODEL

kernel_name = 'hybrid_window_gqa_conformer_convffn_encoder'


def rmsnorm(x, g):
    xf = x.astype(jnp.float32)
    y = xf * lax.rsqrt(jnp.mean(xf * xf, axis=-1, keepdims=True) + EPS)
    return (y * g.astype(jnp.float32)).astype(x.dtype)


def layernorm(x, g, b):
    xf = x.astype(jnp.float32)
    mu = jnp.mean(xf, axis=-1, keepdims=True)
    var = jnp.mean(jnp.square(xf - mu), axis=-1, keepdims=True)
    y = (xf - mu) * lax.rsqrt(var + EPS)
    return (y * g.astype(jnp.float32) + b.astype(jnp.float32)).astype(x.dtype)


def partial_rope(x, positions):
    half = ROT_DIM // 2
    inv = ROPE_THETA ** (-jnp.arange(half, dtype=jnp.float32) / half)
    ang = positions.astype(jnp.float32)[:, None] * inv[None, :]
    cos = jnp.cos(ang)[:, None, :]
    sin = jnp.sin(ang)[:, None, :]
    xf = x.astype(jnp.float32)
    x1 = xf[..., :half]
    x2 = xf[..., half:ROT_DIM]
    out = jnp.concatenate([x1 * cos - x2 * sin, x2 * cos + x1 * sin, xf[..., ROT_DIM:]], axis=-1)
    return out.astype(x.dtype)


def depthwise_conv(x, w, b):
    K = w.shape[0]
    pad = (K - 1) // 2
    y = lax.conv_general_dilated(x, w[:, None, :].astype(x.dtype), window_strides=(1,),
                                 padding=[(pad, pad)], dimension_numbers=('NWC', 'WIO', 'NWC'),
                                 feature_group_count=x.shape[-1])
    return y + b.astype(x.dtype)


def banded_window_attention(q, k, v, sink):
    B, S = q.shape[0], q.shape[1]
    nb = S // BLOCK
    qb = q.reshape(B, nb, BLOCK, N_KV_HEADS, GROUP, HEAD_DIM)

    def windows(t):
        tp = jnp.pad(t, ((0, 0), (BLOCK, BLOCK), (0, 0), (0, 0)))
        tp = tp.reshape(B, nb + 2, BLOCK, N_KV_HEADS, HEAD_DIM)
        return jnp.concatenate([tp[:, :-2], tp[:, 1:-1], tp[:, 2:]], axis=2)

    kw = windows(k)
    vw = windows(v)
    scale = HEAD_DIM ** -0.5
    s = jnp.einsum('bnqhgd,bnkhd->bnhgqk', qb, kw, preferred_element_type=jnp.float32) * scale
    blk = jnp.arange(nb)[:, None, None] * BLOCK
    qpos = blk + jnp.arange(BLOCK)[None, :, None]
    kpos = blk - BLOCK + jnp.arange(3 * BLOCK)[None, None, :]
    valid = (jnp.abs(kpos - qpos) <= WINDOW) & (kpos >= 0) & (kpos < S)
    s = jnp.where(valid[None, :, None, None], s, NEG)
    sk = sink.astype(jnp.float32).reshape(1, 1, N_KV_HEADS, GROUP, 1, 1)
    m = jnp.maximum(jnp.max(s, axis=-1, keepdims=True), sk)
    p = jnp.exp(s - m)
    p = p / (jnp.sum(p, axis=-1, keepdims=True) + jnp.exp(sk - m))
    o = jnp.einsum('bnhgqk,bnkhd->bnqhgd', p.astype(v.dtype), vw)
    return o.reshape(B, S, Q_W)


def encoder_layer(x, norm_mix, w_in, q_norm, k_norm, sink, w_attn_out, conv_dw_w, conv_dw_b,
                  conv_ln_g, conv_ln_b, w_conv_out, w_o, norm_ffn, w_up, ffn_dw_w, ffn_dw_b, w_down):
    B, S, _ = x.shape
    h = rmsnorm(x, norm_mix)
    z = h @ w_in
    o1 = Q_W
    o2 = o1 + KV_W
    o3 = o2 + KV_W
    o4 = o3 + 2 * CONV_CH
    q = z[..., :o1].reshape(B, S, N_HEADS, HEAD_DIM)
    k = z[..., o1:o2].reshape(B, S, N_KV_HEADS, HEAD_DIM)
    v = z[..., o2:o3].reshape(B, S, N_KV_HEADS, HEAD_DIM)
    conv_in = z[..., o3:o4]
    gates = jax.nn.sigmoid(z[..., o4:])
    pos = jnp.arange(S)
    q = partial_rope(rmsnorm(q, q_norm), pos)
    k = partial_rope(rmsnorm(k, k_norm), pos)
    branch_a = banded_window_attention(q, k, v, sink) @ w_attn_out
    c = conv_in[..., :CONV_CH] * jax.nn.sigmoid(conv_in[..., CONV_CH:])
    c = depthwise_conv(c, conv_dw_w, conv_dw_b)
    c = jax.nn.silu(layernorm(c, conv_ln_g, conv_ln_b))
    branch_b = c @ w_conv_out
    merged = gates[..., :D_MODEL] * branch_a + gates[..., D_MODEL:] * branch_b
    x = x + merged @ w_o
    u = rmsnorm(x, norm_ffn) @ w_up
    u = depthwise_conv(u, ffn_dw_w, ffn_dw_b)
    x = x + (jax.nn.silu(u[..., :D_FF]) * u[..., D_FF:]) @ w_down
    return x


def setup_inputs(seed: int = 0) -> dict:
    key = jax.random.key(seed)
    ks = jax.random.split(key, 20)
    f32 = jnp.float32

    def dense(k, shape, fan_in):
        return jax.random.normal(k, shape, f32) * (fan_in ** -0.5)

    def gain(k, shape):
        return 1.0 + 0.02 * jax.random.normal(k, shape, f32)

    def bias(k, shape):
        return 0.02 * jax.random.normal(k, shape, f32)

    L = DEPTH
    return {
        'x_prompt': jax.random.normal(ks[0], (BATCH, SEQ, D_MODEL), f32),
        'x_sample': jax.random.normal(ks[1], (DEC_BATCH, DEC_SEQ, D_MODEL), f32),
        'norm_mix': gain(ks[2], (L, D_MODEL)),
        'w_in': dense(ks[3], (L, D_MODEL, IN_W), D_MODEL),
        'q_norm': gain(ks[4], (L, HEAD_DIM)),
        'k_norm': gain(ks[5], (L, HEAD_DIM)),
        'sink': 0.5 * jax.random.normal(ks[6], (L, N_HEADS), f32),
        'w_attn_out': dense(ks[7], (L, Q_W, D_MODEL), Q_W),
        'conv_dw_w': dense(ks[8], (L, CONV_K, CONV_CH), CONV_K),
        'conv_dw_b': bias(ks[9], (L, CONV_CH)),
        'conv_ln_g': gain(ks[10], (L, CONV_CH)),
        'conv_ln_b': bias(ks[11], (L, CONV_CH)),
        'w_conv_out': dense(ks[12], (L, CONV_CH, D_MODEL), CONV_CH),
        'w_o': dense(ks[13], (L, D_MODEL, D_MODEL), D_MODEL),
        'norm_ffn': gain(ks[14], (L, D_MODEL)),
        'w_up': dense(ks[15], (L, D_MODEL, 2 * D_FF), D_MODEL),
        'ffn_dw_w': dense(ks[16], (L, FFN_K, 2 * D_FF), FFN_K),
        'ffn_dw_b': bias(ks[17], (L, 2 * D_FF)),
        'w_down': dense(ks[18], (L, D_FF, D_MODEL), D_FF),
    }


def reference(x_prompt, x_sample, norm_mix, w_in, q_norm, k_norm, sink, w_attn_out, conv_dw_w,
              conv_dw_b, conv_ln_g, conv_ln_b, w_conv_out, w_o, norm_ffn, w_up, ffn_dw_w, ffn_dw_b,
              w_down):
    def trunk(x):
        for l in range(DEPTH):
            x = encoder_layer(x, norm_mix[l], w_in[l], q_norm[l], k_norm[l], sink[l], w_attn_out[l],
                              conv_dw_w[l], conv_dw_b[l], conv_ln_g[l], conv_ln_b[l], w_conv_out[l],
                              w_o[l], norm_ffn[l], w_up[l], ffn_dw_w[l], ffn_dw_b[l], w_down[l])
        return x

    y_prompt = trunk(x_prompt)
    y_sample = trunk(x_sample)
    return (y_prompt, y_sample)
```

```python
import functools

import jax
import jax.numpy as jnp
import numpy as np
from jax import lax
from jax.experimental import pallas as pl
from jax.experimental.pallas import tpu as pltpu

D_MODEL = 1024
N_HEADS = 8
N_KV_HEADS = 2
HEAD_DIM = 64
ROT_DIM = HEAD_DIM // 4
ROPE_THETA = 500000.0
WINDOW = 128
CONV_CH = 512
CONV_K = 31
D_FF = 2816
FFN_K = 3
EPS = 1e-6
NEG = -1e30
Q_W = N_HEADS * HEAD_DIM

LANES = 128
SUBLANES = 8
BF16_ROWS = 16
VMEM_LIMIT = 56 * 1024 * 1024

TILE = 512
CONV_HALO = 16
FFN_HALO = 16
FF_CHUNK = 256
CONV_ROWS = 32

BF16 = jnp.bfloat16
F32 = jnp.float32


def _dot(a, b):
    return jnp.dot(a, b, preferred_element_type=F32)


def _sigmoid(x):
    return 1.0 / (1.0 + jnp.exp(-x))


def _inproj_kernel(x_ref, gmix_ref, w_ref, gq_ref, gk_ref, cos_ref, sin_ref, bm_ref,
                   q_ref, kk_ref, vv_ref, c_ref, g_ref):
    x = x_ref[...]
    ms = jnp.mean(x * x, axis=-1, keepdims=True)
    h = (x * lax.rsqrt(ms + EPS) * gmix_ref[...]).astype(BF16)

    cos = cos_ref[...]
    sin = sin_ref[...]
    lane = lax.broadcasted_iota(jnp.int32, cos.shape, 1)
    first_half = (lane & (HEAD_DIM - 1)) < (ROT_DIM // 2)
    bm = bm_ref[...]

    def norm_rope(t, gain):
        tt = t * t
        hi = tt.astype(BF16)
        lo = (tt - hi.astype(F32)).astype(BF16)
        msq = _dot(jnp.concatenate([hi, lo], axis=1), bm)
        tn = t * lax.rsqrt(msq + EPS) * gain
        partner = jnp.where(first_half, pltpu.roll(tn, LANES - ROT_DIM // 2, 1),
                            pltpu.roll(tn, ROT_DIM // 2, 1))
        return tn * cos + partner * sin

    zq = _dot(h, w_ref[:, 0:Q_W])
    for j in range(Q_W // LANES):
        sl = slice(j * LANES, (j + 1) * LANES)
        q_ref[:, sl] = (norm_rope(zq[:, sl], gq_ref[:, sl]) * (HEAD_DIM ** -0.5)).astype(BF16)

    zk = _dot(h, w_ref[:, Q_W:Q_W + 2 * LANES])
    for j in range(2):
        sl = slice(j * LANES, (j + 1) * LANES)
        kk_ref[:, sl] = norm_rope(zk[:, sl], gk_ref[:, sl]).astype(BF16)

    o_v = Q_W + 2 * LANES
    vv_ref[...] = _dot(h, w_ref[:, o_v:o_v + 2 * LANES]).astype(BF16)

    o_c = o_v + 2 * LANES
    za = _dot(h, w_ref[:, o_c:o_c + CONV_CH])
    zb = _dot(h, w_ref[:, o_c + CONV_CH:o_c + 2 * CONV_CH])
    c_ref[...] = (za * _sigmoid(zb)).astype(BF16)

    o_g = o_c + 2 * CONV_CH
    for j in range(2 * D_MODEL // 512):
        zg = _dot(h, w_ref[:, o_g + j * 512:o_g + (j + 1) * 512])
        g_ref[:, j * 512:(j + 1) * 512] = _sigmoid(zg).astype(BF16)


def _inproj(x, gmix, w_in, gq, gk, cos_t, sin_t, bm, seq):
    t = x.shape[0]
    nts = seq // TILE
    w_cols = w_in.shape[1]
    row = lambda i: (i, 0)
    const = lambda i: (0, 0)
    pos = lambda i: (i % nts, 0)
    return pl.pallas_call(
        _inproj_kernel,
        grid=(t // TILE,),
        in_specs=[
            pl.BlockSpec((TILE, D_MODEL), row),
            pl.BlockSpec((1, D_MODEL), const),
            pl.BlockSpec((D_MODEL, w_cols), const, pipeline_mode=pl.Buffered(1)),
            pl.BlockSpec((1, Q_W), const),
            pl.BlockSpec((1, 2 * LANES), const),
            pl.BlockSpec((TILE, LANES), pos),
            pl.BlockSpec((TILE, LANES), pos),
            pl.BlockSpec((2 * LANES, LANES), const),
        ],
        out_specs=[
            pl.BlockSpec((TILE, Q_W), row),
            pl.BlockSpec((TILE, 2 * LANES), row),
            pl.BlockSpec((TILE, 2 * LANES), row),
            pl.BlockSpec((TILE, CONV_CH), row),
            pl.BlockSpec((TILE, 2 * D_MODEL), row),
        ],
        out_shape=[
            jax.ShapeDtypeStruct((t, Q_W), BF16),
            jax.ShapeDtypeStruct((t, 2 * LANES), BF16),
            jax.ShapeDtypeStruct((t, 2 * LANES), BF16),
            jax.ShapeDtypeStruct((t, CONV_CH), BF16),
            jax.ShapeDtypeStruct((t, 2 * D_MODEL), BF16),
        ],
        compiler_params=pltpu.CompilerParams(
            dimension_semantics=("arbitrary",), vmem_limit_bytes=VMEM_LIMIT),
        name="inproj",
    )(x, gmix, w_in, gq, gk, cos_t, sin_t, bm)


def _mixer_kernel(nts, sink_ref, x_ref, q_ref, kc_ref, kp_ref, kn_ref, vc_ref, vp_ref, vn_ref,
                  c_ref, cp_ref, cn_ref, g_ref, band_ref, wa_ref, dww_ref, dwb_ref,
                  lng_ref, lnb_ref, wc_ref, wo_ref, o_ref, kbuf, vbuf, cbuf, attn_buf, act_buf):
    i = pl.program_id(0)
    j = lax.rem(i, nts)
    is_first = j == 0
    is_last = j == nts - 1
    nb = TILE // WINDOW

    kbuf[0:WINDOW, :] = kp_ref[...]
    kbuf[WINDOW:WINDOW + TILE, :] = kc_ref[...]
    kbuf[WINDOW + TILE:, :] = kn_ref[...]
    vbuf[0:WINDOW, :] = vp_ref[...]
    vbuf[WINDOW:WINDOW + TILE, :] = vc_ref[...]
    vbuf[WINDOW + TILE:, :] = vn_ref[...]

    edge_p = jnp.where(is_first, NEG, 0.0).astype(F32)
    edge_n = jnp.where(is_last, NEG, 0.0).astype(F32)
    band = band_ref[...]
    lo64 = lax.broadcasted_iota(jnp.int32, (WINDOW, LANES), 1) < HEAD_DIM
    lo64_win = lax.broadcasted_iota(jnp.int32, (3 * WINDOW, LANES), 1) < HEAD_DIM
    one = jnp.ones((), BF16)
    zero = jnp.zeros((), BF16)

    for b in range(nb):
        bias = band
        if b == 0:
            bias = jnp.concatenate([bias[:, :WINDOW] + edge_p, bias[:, WINDOW:]], axis=1)
        if b == nb - 1:
            bias = jnp.concatenate([bias[:, :2 * WINDOW], bias[:, 2 * WINDOW:] + edge_n], axis=1)
        rows = slice(b * WINDOW, (b + 1) * WINDOW)
        win = slice(b * WINDOW, (b + 3) * WINDOW)
        v_a = vbuf[win, 0:LANES]
        v_b = vbuf[win, LANES:2 * LANES]
        for kvh in range(N_KV_HEADS):
            kwin = kbuf[win, kvh * LANES:(kvh + 1) * LANES]
            if kvh == 0:
                v_even = jnp.where(lo64_win, v_a, one)
                v_odd = jnp.where(lo64_win, one, v_b)
            else:
                v_even = jnp.where(lo64_win, v_b, one)
                v_odd = jnp.where(lo64_win, one, v_a)
            base = kvh * 2 * LANES
            qp0 = q_ref[rows, base:base + LANES]
            qp1 = q_ref[rows, base + LANES:base + 2 * LANES]
            lhs = jnp.concatenate([
                jnp.where(lo64, qp0, zero), jnp.where(lo64, qp1, zero),
                jnp.where(lo64, zero, qp0), jnp.where(lo64, zero, qp1)], axis=0)
            s = lax.dot_general(lhs, kwin, (((1,), (1,)), ((), ())), preferred_element_type=F32)
            heads = [4 * kvh, 4 * kvh + 2, 4 * kvh + 1, 4 * kvh + 3]
            ps = []
            es = []
            for r, hd in enumerate(heads):
                sh = s[r * WINDOW:(r + 1) * WINDOW, :] + bias
                sk = sink_ref[hd]
                m = jnp.maximum(jnp.max(sh, axis=-1, keepdims=True), sk)
                ps.append(jnp.exp(sh - m).astype(BF16))
                es.append(jnp.exp(sk - m))
            r_even = _dot(jnp.concatenate(ps[0:2], axis=0), v_even)
            r_odd = _dot(jnp.concatenate(ps[2:4], axis=0), v_odd)
            for pr in range(2):
                re = r_even[pr * WINDOW:(pr + 1) * WINDOW, :]
                ro = r_odd[pr * WINDOW:(pr + 1) * WINDOW, :]
                num = jnp.where(lo64, re, ro)
                den = pltpu.roll(jnp.where(lo64, ro, re), HEAD_DIM, 1)
                den = den + jnp.where(lo64, es[pr], es[2 + pr])
                col = (2 * kvh + pr) * LANES
                attn_buf[rows, col:col + LANES] = (num / den).astype(BF16)

    keep_p = jnp.where(is_first, 0.0, 1.0).astype(F32)
    keep_n = jnp.where(is_last, 0.0, 1.0).astype(F32)
    cbuf[0:CONV_HALO, :] = cp_ref[...].astype(F32) * keep_p
    cbuf[CONV_HALO:CONV_HALO + TILE, :] = c_ref[...].astype(F32)
    cbuf[CONV_HALO + TILE:, :] = cn_ref[...].astype(F32) * keep_n
    pad = (CONV_K - 1) // 2
    groups = CONV_ROWS // SUBLANES
    dwb = dwb_ref[...]
    lng = lng_ref[...]
    lnb = lnb_ref[...]
    for ch in range(TILE // CONV_ROWS):
        r0 = ch * CONV_ROWS
        acc = [None] * groups
        for k in range(CONV_K):
            wk = dww_ref[k * SUBLANES:(k + 1) * SUBLANES, :]
            for gi in range(groups):
                st = r0 + gi * SUBLANES + CONV_HALO - pad + k
                term = cbuf[st:st + SUBLANES, :] * wk
                acc[gi] = term if acc[gi] is None else acc[gi] + term
        y = jnp.concatenate(acc, axis=0) + dwb
        mu = jnp.mean(y, axis=-1, keepdims=True)
        yc = y - mu
        var = jnp.mean(yc * yc, axis=-1, keepdims=True)
        yn = yc * lax.rsqrt(var + EPS) * lng + lnb
        act_buf[r0:r0 + CONV_ROWS, :] = (yn * _sigmoid(yn)).astype(BF16)

    for b in range(nb):
        rows = slice(b * WINDOW, (b + 1) * WINDOW)
        br_a = _dot(attn_buf[rows, :], wa_ref[...])
        br_b = _dot(act_buf[rows, :], wc_ref[...])
        merged = (g_ref[rows, 0:D_MODEL].astype(F32) * br_a
                  + g_ref[rows, D_MODEL:].astype(F32) * br_b)
        o_ref[rows, :] = x_ref[rows, :] + _dot(merged.astype(BF16), wo_ref[...])


def _mixer(x, q, kk, vv, c, g, sink, band, wa, dww8, dwb, lng, lnb, wc, wo, seq):
    t = x.shape[0]
    nts = seq // TILE
    kb = TILE // WINDOW
    cb = TILE // CONV_HALO
    row = lambda i: (i, 0)
    const = lambda i: (0, 0)
    kprev = lambda i: (jnp.maximum(i * kb - 1, 0), 0)
    knext = lambda i: (jnp.minimum((i + 1) * kb, t // WINDOW - 1), 0)
    cprev = lambda i: (jnp.maximum(i * cb - 1, 0), 0)
    cnext = lambda i: (jnp.minimum((i + 1) * cb, t // CONV_HALO - 1), 0)
    single = pl.Buffered(1)
    return pl.pallas_call(
        functools.partial(_mixer_kernel, nts),
        grid=(t // TILE,),
        in_specs=[
            pl.BlockSpec(memory_space=pltpu.SMEM),
            pl.BlockSpec((TILE, D_MODEL), row),
            pl.BlockSpec((TILE, Q_W), row),
            pl.BlockSpec((TILE, 2 * LANES), row),
            pl.BlockSpec((WINDOW, 2 * LANES), kprev),
            pl.BlockSpec((WINDOW, 2 * LANES), knext),
            pl.BlockSpec((TILE, 2 * LANES), row),
            pl.BlockSpec((WINDOW, 2 * LANES), kprev),
            pl.BlockSpec((WINDOW, 2 * LANES), knext),
            pl.BlockSpec((TILE, CONV_CH), row),
            pl.BlockSpec((CONV_HALO, CONV_CH), cprev),
            pl.BlockSpec((CONV_HALO, CONV_CH), cnext),
            pl.BlockSpec((TILE, 2 * D_MODEL), row),
            pl.BlockSpec((WINDOW, 3 * WINDOW), const),
            pl.BlockSpec((Q_W, D_MODEL), const, pipeline_mode=single),
            pl.BlockSpec((CONV_K * SUBLANES, CONV_CH), const),
            pl.BlockSpec((1, CONV_CH), const),
            pl.BlockSpec((1, CONV_CH), const),
            pl.BlockSpec((1, CONV_CH), const),
            pl.BlockSpec((CONV_CH, D_MODEL), const, pipeline_mode=single),
            pl.BlockSpec((D_MODEL, D_MODEL), const, pipeline_mode=single),
        ],
        out_specs=pl.BlockSpec((TILE, D_MODEL), row),
        out_shape=jax.ShapeDtypeStruct((t, D_MODEL), F32),
        scratch_shapes=[
            pltpu.VMEM((TILE + 2 * WINDOW, 2 * LANES), BF16),
            pltpu.VMEM((TILE + 2 * WINDOW, 2 * LANES), BF16),
            pltpu.VMEM((TILE + 2 * CONV_HALO, CONV_CH), F32),
            pltpu.VMEM((TILE, Q_W), BF16),
            pltpu.VMEM((TILE, CONV_CH), BF16),
        ],
        compiler_params=pltpu.CompilerParams(
            dimension_semantics=("arbitrary",), vmem_limit_bytes=VMEM_LIMIT),
        name="mixer",
    )(sink, x, q, kk, kk, kk, vv, vv, vv, c, c, c, g, band, wa, dww8, dwb, lng, lnb, wc, wo)


def _ffn_kernel(nts, x_ref, xp_ref, xn_ref, gn_ref, wup_ref, dw_ref, db_ref, wdn_ref, o_ref,
                hbuf, ubuf):
    i = pl.program_id(0)
    j = lax.rem(i, nts)
    keep_p = jnp.where(j == 0, 0.0, 1.0).astype(F32)
    keep_n = jnp.where(j == nts - 1, 0.0, 1.0).astype(F32)
    gn = gn_ref[...]

    def normed(x):
        ms = jnp.mean(x * x, axis=-1, keepdims=True)
        return x * lax.rsqrt(ms + EPS) * gn

    hbuf[0:FFN_HALO, :] = (normed(xp_ref[...]) * keep_p).astype(BF16)
    hbuf[FFN_HALO:FFN_HALO + TILE, :] = normed(x_ref[...]).astype(BF16)
    hbuf[FFN_HALO + TILE:, :] = (normed(xn_ref[...]) * keep_n).astype(BF16)

    pad = (FFN_K - 1) // 2
    n_chunks = D_FF // FF_CHUNK

    def conv(buf, col):
        y = db_ref[:, col:col + FF_CHUNK]
        for k in range(FFN_K):
            st = FFN_HALO - pad + k
            y = y + buf[st:st + TILE, :] * dw_ref[k:k + 1, col:col + FF_CHUNK]
        return y

    for ci in range(n_chunks):
        c0 = ci * FF_CHUNK
        slot = ci % 2
        ua = ubuf.at[2 * slot]
        ub = ubuf.at[2 * slot + 1]
        h = hbuf[...]
        ua[...] = _dot(h, wup_ref[:, c0:c0 + FF_CHUNK])
        ub[...] = _dot(h, wup_ref[:, D_FF + c0:D_FF + c0 + FF_CHUNK])
        ya = conv(ua, c0)
        yb = conv(ub, D_FF + c0)
        act = (ya * _sigmoid(ya) * yb).astype(BF16)
        part = _dot(act, wdn_ref[c0:c0 + FF_CHUNK, :])
        if ci == 0:
            o_ref[...] = x_ref[...] + part
        else:
            o_ref[...] += part


def _ffn(x, gn, wup, dw, db, wdn, seq):
    t = x.shape[0]
    nts = seq // TILE
    hb = TILE // FFN_HALO
    row = lambda i: (i, 0)
    const = lambda i: (0, 0)
    prev = lambda i: (jnp.maximum(i * hb - 1, 0), 0)
    nxt = lambda i: (jnp.minimum((i + 1) * hb, t // FFN_HALO - 1), 0)
    single = pl.Buffered(1)
    return pl.pallas_call(
        functools.partial(_ffn_kernel, nts),
        grid=(t // TILE,),
        in_specs=[
            pl.BlockSpec((TILE, D_MODEL), row),
            pl.BlockSpec((FFN_HALO, D_MODEL), prev),
            pl.BlockSpec((FFN_HALO, D_MODEL), nxt),
            pl.BlockSpec((1, D_MODEL), const),
            pl.BlockSpec((D_MODEL, 2 * D_FF), const, pipeline_mode=single),
            pl.BlockSpec((FFN_K, 2 * D_FF), const),
            pl.BlockSpec((1, 2 * D_FF), const),
            pl.BlockSpec((D_FF, D_MODEL), const, pipeline_mode=single),
        ],
        out_specs=pl.BlockSpec((TILE, D_MODEL), row),
        out_shape=jax.ShapeDtypeStruct((t, D_MODEL), F32),
        scratch_shapes=[
            pltpu.VMEM((TILE + 2 * FFN_HALO, D_MODEL), BF16),
            pltpu.VMEM((4, TILE + 2 * FFN_HALO, FF_CHUNK), F32),
        ],
        compiler_params=pltpu.CompilerParams(
            dimension_semantics=("arbitrary",), vmem_limit_bytes=VMEM_LIMIT),
        name="ffn",
    )(x, x, x, gn, wup, dw, db, wdn)


def _rope_tables(seq):
    half = ROT_DIM // 2
    inv = ROPE_THETA ** (-jnp.arange(half, dtype=F32) / half)
    ang = jnp.arange(seq, dtype=F32)[:, None] * inv[None, :]
    cos = jnp.cos(ang)
    sin = jnp.sin(ang)
    ones = jnp.ones((seq, HEAD_DIM - ROT_DIM), F32)
    zeros = jnp.zeros((seq, HEAD_DIM - ROT_DIM), F32)
    cos_h = jnp.concatenate([cos, cos, ones], axis=1)
    sin_h = jnp.concatenate([-sin, sin, zeros], axis=1)
    reps = LANES // HEAD_DIM
    return jnp.tile(cos_h, (1, reps)), jnp.tile(sin_h, (1, reps))


def _band_bias():
    qi = np.arange(WINDOW)[:, None]
    kj = np.arange(3 * WINDOW)[None, :] - WINDOW
    return jnp.asarray(np.where(np.abs(kj - qi) <= WINDOW, 0.0, NEG), F32)


def _head_mean_matrix():
    r = np.arange(2 * LANES)[:, None] % LANES
    c = np.arange(LANES)[None, :]
    return jnp.asarray(np.where(r // HEAD_DIM == c // HEAD_DIM, 1.0 / HEAD_DIM, 0.0), BF16)


def _prep_layer(norm_mix, w_in, q_norm, k_norm, sink, w_attn_out, conv_dw_w, conv_dw_b,
                conv_ln_g, conv_ln_b, w_conv_out, w_o, norm_ffn, w_up, ffn_dw_w, ffn_dw_b, w_down):
    o1 = Q_W
    o2 = o1 + N_KV_HEADS * HEAD_DIM
    o3 = o2 + N_KV_HEADS * HEAD_DIM
    k0, k1 = w_in[:, o1:o1 + HEAD_DIM], w_in[:, o1 + HEAD_DIM:o2]
    v0, v1 = w_in[:, o2:o2 + HEAD_DIM], w_in[:, o2 + HEAD_DIM:o3]
    w_cat = jnp.concatenate([w_in[:, :o1], k0, k0, k1, k1, v0, v1, v1, v0, w_in[:, o3:]], axis=1)
    return dict(
        gmix=norm_mix[None, :],
        w_in=w_cat.astype(BF16),
        gq=jnp.tile(q_norm, N_HEADS)[None, :],
        gk=jnp.tile(k_norm, 2 * N_KV_HEADS)[None, :],
        sink=sink,
        wa=w_attn_out.astype(BF16),
        dww8=jnp.repeat(conv_dw_w, SUBLANES, axis=0),
        dwb=conv_dw_b[None, :],
        lng=conv_ln_g[None, :],
        lnb=conv_ln_b[None, :],
        wc=w_conv_out.astype(BF16),
        wo=w_o.astype(BF16),
        gn=norm_ffn[None, :],
        wup=w_up.astype(BF16),
        dw=ffn_dw_w,
        db=ffn_dw_b[None, :],
        wdn=w_down.astype(BF16),
    )


def kernel(x_prompt, x_sample, norm_mix, w_in, q_norm, k_norm, sink, w_attn_out, conv_dw_w,
           conv_dw_b, conv_ln_g, conv_ln_b, w_conv_out, w_o, norm_ffn, w_up, ffn_dw_w, ffn_dw_b,
           w_down):
    depth = w_in.shape[0]
    stacked = (norm_mix, w_in, q_norm, k_norm, sink, w_attn_out, conv_dw_w, conv_dw_b, conv_ln_g,
               conv_ln_b, w_conv_out, w_o, norm_ffn, w_up, ffn_dw_w, ffn_dw_b, w_down)
    layers = [_prep_layer(*(a[l] for a in stacked)) for l in range(depth)]
    band = _band_bias()
    bm = _head_mean_matrix()

    def trunk(x):
        b, seq, d = x.shape
        assert d == D_MODEL and seq % TILE == 0
        cos_t, sin_t = _rope_tables(seq)
        y = x.reshape(b * seq, d)
        for p in layers:
            q, kk, vv, c, g = _inproj(y, p["gmix"], p["w_in"], p["gq"], p["gk"], cos_t, sin_t,
                                      bm, seq)
            y = _mixer(y, q, kk, vv, c, g, p["sink"], band, p["wa"], p["dww8"], p["dwb"],
                       p["lng"], p["lnb"], p["wc"], p["wo"], seq)
            y = _ffn(y, p["gn"], p["wup"], p["dw"], p["db"], p["wdn"], seq)
        return y.reshape(b, seq, d)

    return (trunk(x_prompt), trunk(x_sample))
```

```python
import functools

import jax
import jax.numpy as jnp
import numpy as np
from jax import lax
from jax.experimental import pallas as pl
from jax.experimental.pallas import tpu as pltpu

D_MODEL = 1024
N_HEADS = 8
N_KV_HEADS = 2
HEAD_DIM = 64
ROT_DIM = HEAD_DIM // 4
ROPE_THETA = 500000.0
WINDOW = 128
CONV_CH = 512
CONV_K = 31
D_FF = 2816
FFN_K = 3
EPS = 1e-6
NEG = -1e30
Q_W = N_HEADS * HEAD_DIM

LANES = 128
SUBLANES = 8
MXU_DIM = 256
VMEM_LIMIT = 56 * 1024 * 1024

TILE = 512
CONV_HALO = 16
FFN_HALO = 16
FF_CHUNK = 512
CONV_ROWS = 32

BF16 = jnp.bfloat16
F32 = jnp.float32


def _dot(a, b):
    return jnp.dot(a, b, preferred_element_type=F32)


def _sigmoid(x):
    return 1.0 / (1.0 + jnp.exp(-x))


def _inproj_kernel(nts, x_ref, xp_ref, xn_ref, gmix_ref, w_ref, gq_ref, gk_ref, cos_ref, sin_ref,
                   bm_ref, dww_ref, dwb_ref, lng_ref, lnb_ref,
                   q_ref, kk_ref, vv_ref, a_ref, g_ref, hbuf, cshift):
    i = pl.program_id(0)
    j = lax.rem(i, nts)
    keep_p = jnp.where(j == 0, 0.0, 1.0).astype(F32)
    keep_n = jnp.where(j == nts - 1, 0.0, 1.0).astype(F32)
    gmix = gmix_ref[...]

    def normed(x):
        ms = jnp.mean(x * x, axis=-1, keepdims=True)
        return x * lax.rsqrt(ms + EPS) * gmix

    hbuf[0:CONV_HALO, :] = (normed(xp_ref[...]) * keep_p).astype(BF16)
    hbuf[CONV_HALO:CONV_HALO + TILE, :] = normed(x_ref[...]).astype(BF16)
    hbuf[CONV_HALO + TILE:, :] = (normed(xn_ref[...]) * keep_n).astype(BF16)

    o_v = Q_W + 2 * LANES
    o_c = o_v + 2 * LANES
    o_g = o_c + 2 * CONV_CH

    h_all = hbuf[...]
    za = _dot(h_all, w_ref[:, o_c:o_c + CONV_CH])
    zb = _dot(h_all, w_ref[:, o_c + CONV_CH:o_c + 2 * CONV_CH])
    glu = za * _sigmoid(zb)
    glu = jnp.concatenate([glu, jnp.zeros((SUBLANES, CONV_CH), F32)], axis=0)
    rows_all = TILE + 2 * CONV_HALO
    for r in range(SUBLANES):
        cshift[r] = glu[r:r + rows_all, :]

    pad = (CONV_K - 1) // 2
    groups = CONV_ROWS // SUBLANES
    dwb = dwb_ref[...]
    lng = lng_ref[...]
    lnb = lnb_ref[...]

    def conv_chunk(ch):
        r0 = ch * CONV_ROWS
        acc = [None] * groups
        for k in range(CONV_K):
            wk = dww_ref[k * SUBLANES:(k + 1) * SUBLANES, :]
            a8, r = divmod(CONV_HALO - pad + k, SUBLANES)
            for gi in range(groups):
                st = r0 + (gi + a8) * SUBLANES
                term = cshift[r, st:st + SUBLANES, :] * wk
                acc[gi] = term if acc[gi] is None else acc[gi] + term
        y = jnp.concatenate(acc, axis=0) + dwb
        mu = jnp.mean(y, axis=-1, keepdims=True)
        yc = y - mu
        var = jnp.mean(yc * yc, axis=-1, keepdims=True)
        yn = yc * lax.rsqrt(var + EPS) * lng + lnb
        a_ref[r0:r0 + CONV_ROWS, :] = (yn * _sigmoid(yn)).astype(BF16)

    n_conv = TILE // CONV_ROWS
    conv_done = [0]

    def conv_some(n):
        for ch in range(conv_done[0], min(conv_done[0] + n, n_conv)):
            conv_chunk(ch)
        conv_done[0] = min(conv_done[0] + n, n_conv)

    h = hbuf[CONV_HALO:CONV_HALO + TILE, :]
    cos = cos_ref[...]
    sin = sin_ref[...]
    lane = lax.broadcasted_iota(jnp.int32, cos.shape, 1)
    first_half = (lane & (HEAD_DIM - 1)) < (ROT_DIM // 2)
    bm = bm_ref[...]

    def norm_rope(t, gain):
        tt = t * t
        hi = tt.astype(BF16)
        lo = (tt - hi.astype(F32)).astype(BF16)
        msq = _dot(jnp.concatenate([hi, lo], axis=1), bm)
        tn = t * lax.rsqrt(msq + EPS) * gain
        partner = jnp.where(first_half, pltpu.roll(tn, LANES - ROT_DIM // 2, 1),
                            pltpu.roll(tn, ROT_DIM // 2, 1))
        return tn * cos + partner * sin

    n_gate = 2 * D_MODEL // 512
    per_step = -(-n_conv // (n_gate + 2))

    zq = _dot(h, w_ref[:, 0:Q_W])
    for jq in range(Q_W // LANES):
        sl = slice(jq * LANES, (jq + 1) * LANES)
        q_ref[:, sl] = (norm_rope(zq[:, sl], gq_ref[:, sl]) * (HEAD_DIM ** -0.5)).astype(BF16)
    conv_some(per_step)

    zk = _dot(h, w_ref[:, Q_W:Q_W + 2 * LANES])
    for jk in range(2):
        sl = slice(jk * LANES, (jk + 1) * LANES)
        kk_ref[:, sl] = norm_rope(zk[:, sl], gk_ref[:, sl]).astype(BF16)
    vv_ref[...] = _dot(h, w_ref[:, o_v:o_v + 2 * LANES]).astype(BF16)
    conv_some(per_step)

    for jg in range(n_gate):
        zg = _dot(h, w_ref[:, o_g + jg * 512:o_g + (jg + 1) * 512])
        g_ref[:, jg * 512:(jg + 1) * 512] = _sigmoid(zg).astype(BF16)
        conv_some(per_step)
    conv_some(n_conv)


def _inproj(x, gmix, w_in, gq, gk, cos_t, sin_t, bm, dww8, dwb, lng, lnb, seq):
    t = x.shape[0]
    nts = seq // TILE
    hb = TILE // CONV_HALO
    w_cols = w_in.shape[1]
    row = lambda i: (i, 0)
    const = lambda i: (0, 0)
    pos = lambda i: (i % nts, 0)
    prev = lambda i: (jnp.maximum(i * hb - 1, 0), 0)
    nxt = lambda i: (jnp.minimum((i + 1) * hb, t // CONV_HALO - 1), 0)
    return pl.pallas_call(
        functools.partial(_inproj_kernel, nts),
        grid=(t // TILE,),
        in_specs=[
            pl.BlockSpec((TILE, D_MODEL), row),
            pl.BlockSpec((CONV_HALO, D_MODEL), prev),
            pl.BlockSpec((CONV_HALO, D_MODEL), nxt),
            pl.BlockSpec((1, D_MODEL), const),
            pl.BlockSpec((D_MODEL, w_cols), const, pipeline_mode=pl.Buffered(1)),
            pl.BlockSpec((1, Q_W), const),
            pl.BlockSpec((1, 2 * LANES), const),
            pl.BlockSpec((TILE, LANES), pos),
            pl.BlockSpec((TILE, LANES), pos),
            pl.BlockSpec((2 * LANES, LANES), const),
            pl.BlockSpec((CONV_K * SUBLANES, CONV_CH), const),
            pl.BlockSpec((1, CONV_CH), const),
            pl.BlockSpec((1, CONV_CH), const),
            pl.BlockSpec((1, CONV_CH), const),
        ],
        out_specs=[
            pl.BlockSpec((TILE, Q_W), row),
            pl.BlockSpec((TILE, 2 * LANES), row),
            pl.BlockSpec((TILE, 2 * LANES), row),
            pl.BlockSpec((TILE, CONV_CH), row),
            pl.BlockSpec((TILE, 2 * D_MODEL), row),
        ],
        out_shape=[
            jax.ShapeDtypeStruct((t, Q_W), BF16),
            jax.ShapeDtypeStruct((t, 2 * LANES), BF16),
            jax.ShapeDtypeStruct((t, 2 * LANES), BF16),
            jax.ShapeDtypeStruct((t, CONV_CH), BF16),
            jax.ShapeDtypeStruct((t, 2 * D_MODEL), BF16),
        ],
        scratch_shapes=[
            pltpu.VMEM((TILE + 2 * CONV_HALO, D_MODEL), BF16),
            pltpu.VMEM((SUBLANES, TILE + 2 * CONV_HALO, CONV_CH), F32),
        ],
        compiler_params=pltpu.CompilerParams(
            dimension_semantics=("arbitrary",), vmem_limit_bytes=VMEM_LIMIT),
        name="inproj",
    )(x, x, x, gmix, w_in, gq, gk, cos_t, sin_t, bm, dww8, dwb, lng, lnb)


def _mixer_kernel(nts, sink_ref, x_ref, q_ref, kc_ref, kp_ref, kn_ref, vc_ref, vp_ref, vn_ref,
                  a_ref, g_ref, band_ref, wa_ref, wc_ref, wo_ref, o_ref,
                  kbuf, vbuf, attn_buf, mbuf):
    i = pl.program_id(0)
    j = lax.rem(i, nts)
    is_first = j == 0
    is_last = j == nts - 1
    nb = TILE // WINDOW

    kbuf[0:WINDOW, :] = kp_ref[...]
    kbuf[WINDOW:WINDOW + TILE, :] = kc_ref[...]
    kbuf[WINDOW + TILE:, :] = kn_ref[...]
    vbuf[0:WINDOW, :] = vp_ref[...]
    vbuf[WINDOW:WINDOW + TILE, :] = vc_ref[...]
    vbuf[WINDOW + TILE:, :] = vn_ref[...]

    edge_p = jnp.where(is_first, NEG, 0.0).astype(F32)
    edge_n = jnp.where(is_last, NEG, 0.0).astype(F32)
    band = band_ref[...]
    lo64 = lax.broadcasted_iota(jnp.int32, (WINDOW, LANES), 1) < HEAD_DIM
    lo64_win = lax.broadcasted_iota(jnp.int32, (3 * WINDOW, LANES), 1) < HEAD_DIM
    one = jnp.ones((), BF16)
    zero = jnp.zeros((), BF16)

    for b in range(nb):
        bias = band
        if b == 0:
            bias = jnp.concatenate([bias[:, :WINDOW] + edge_p, bias[:, WINDOW:]], axis=1)
        if b == nb - 1:
            bias = jnp.concatenate([bias[:, :2 * WINDOW], bias[:, 2 * WINDOW:] + edge_n], axis=1)
        rows = slice(b * WINDOW, (b + 1) * WINDOW)
        win = slice(b * WINDOW, (b + 3) * WINDOW)
        v_a = vbuf[win, 0:LANES]
        v_b = vbuf[win, LANES:2 * LANES]
        for kvh in range(N_KV_HEADS):
            kwin = kbuf[win, kvh * LANES:(kvh + 1) * LANES]
            if kvh == 0:
                v_even = jnp.where(lo64_win, v_a, one)
                v_odd = jnp.where(lo64_win, one, v_b)
            else:
                v_even = jnp.where(lo64_win, v_b, one)
                v_odd = jnp.where(lo64_win, one, v_a)
            base = kvh * 2 * LANES
            qp0 = q_ref[rows, base:base + LANES]
            qp1 = q_ref[rows, base + LANES:base + 2 * LANES]
            lhs = jnp.concatenate([
                jnp.where(lo64, qp0, zero), jnp.where(lo64, qp1, zero),
                jnp.where(lo64, zero, qp0), jnp.where(lo64, zero, qp1)], axis=0)
            s = lax.dot_general(lhs, kwin, (((1,), (1,)), ((), ())), preferred_element_type=F32)
            heads = [4 * kvh, 4 * kvh + 2, 4 * kvh + 1, 4 * kvh + 3]
            ps = []
            es = []
            for r, hd in enumerate(heads):
                sh = s[r * WINDOW:(r + 1) * WINDOW, :] + bias
                sk = sink_ref[hd]
                m = jnp.maximum(jnp.max(sh, axis=-1, keepdims=True), sk)
                ps.append(jnp.exp(sh - m).astype(BF16))
                es.append(jnp.exp(sk - m))
            r_even = _dot(jnp.concatenate(ps[0:2], axis=0), v_even)
            r_odd = _dot(jnp.concatenate(ps[2:4], axis=0), v_odd)
            for pr in range(2):
                re = r_even[pr * WINDOW:(pr + 1) * WINDOW, :]
                ro = r_odd[pr * WINDOW:(pr + 1) * WINDOW, :]
                num = jnp.where(lo64, re, ro)
                den = pltpu.roll(jnp.where(lo64, ro, re), HEAD_DIM, 1)
                den = den + jnp.where(lo64, es[pr], es[2 + pr])
                col = (2 * kvh + pr) * LANES
                attn_buf[rows, col:col + LANES] = (num / den).astype(BF16)

    attn = attn_buf[...]
    act = a_ref[...]
    for n0 in range(0, D_MODEL, MXU_DIM):
        cols = slice(n0, n0 + MXU_DIM)
        gcols = slice(D_MODEL + n0, D_MODEL + n0 + MXU_DIM)
        br_a = _dot(attn, wa_ref[:, cols])
        br_b = _dot(act, wc_ref[:, cols])
        merged = g_ref[:, cols].astype(F32) * br_a + g_ref[:, gcols].astype(F32) * br_b
        mbuf[:, cols] = merged.astype(BF16)
    merged_all = mbuf[...]
    for n0 in range(0, D_MODEL, MXU_DIM):
        cols = slice(n0, n0 + MXU_DIM)
        o_ref[:, cols] = x_ref[:, cols] + _dot(merged_all, wo_ref[:, cols])


def _mixer(x, q, kk, vv, act, g, sink, band, wa, wc, wo, seq):
    t = x.shape[0]
    nts = seq // TILE
    kb = TILE // WINDOW
    row = lambda i: (i, 0)
    const = lambda i: (0, 0)
    kprev = lambda i: (jnp.maximum(i * kb - 1, 0), 0)
    knext = lambda i: (jnp.minimum((i + 1) * kb, t // WINDOW - 1), 0)
    single = pl.Buffered(1)
    return pl.pallas_call(
        functools.partial(_mixer_kernel, nts),
        grid=(t // TILE,),
        in_specs=[
            pl.BlockSpec(memory_space=pltpu.SMEM),
            pl.BlockSpec((TILE, D_MODEL), row),
            pl.BlockSpec((TILE, Q_W), row),
            pl.BlockSpec((TILE, 2 * LANES), row),
            pl.BlockSpec((WINDOW, 2 * LANES), kprev),
            pl.BlockSpec((WINDOW, 2 * LANES), knext),
            pl.BlockSpec((TILE, 2 * LANES), row),
            pl.BlockSpec((WINDOW, 2 * LANES), kprev),
            pl.BlockSpec((WINDOW, 2 * LANES), knext),
            pl.BlockSpec((TILE, CONV_CH), row),
            pl.BlockSpec((TILE, 2 * D_MODEL), row),
            pl.BlockSpec((WINDOW, 3 * WINDOW), const),
            pl.BlockSpec((Q_W, D_MODEL), const, pipeline_mode=single),
            pl.BlockSpec((CONV_CH, D_MODEL), const, pipeline_mode=single),
            pl.BlockSpec((D_MODEL, D_MODEL), const, pipeline_mode=single),
        ],
        out_specs=pl.BlockSpec((TILE, D_MODEL), row),
        out_shape=jax.ShapeDtypeStruct((t, D_MODEL), F32),
        scratch_shapes=[
            pltpu.VMEM((TILE + 2 * WINDOW, 2 * LANES), BF16),
            pltpu.VMEM((TILE + 2 * WINDOW, 2 * LANES), BF16),
            pltpu.VMEM((TILE, Q_W), BF16),
            pltpu.VMEM((TILE, D_MODEL), BF16),
        ],
        compiler_params=pltpu.CompilerParams(
            dimension_semantics=("arbitrary",), vmem_limit_bytes=VMEM_LIMIT),
        name="mixer",
    )(sink, x, q, kk, kk, kk, vv, vv, vv, act, g, band, wa, wc, wo)


def _ffn_chunks():
    out = []
    c0 = 0
    while c0 < D_FF:
        out.append((c0, min(FF_CHUNK, D_FF - c0)))
        c0 += FF_CHUNK
    return out


def _ffn_kernel(nts, x_ref, xp_ref, xn_ref, gn_ref, wup_ref, dw_ref, db_ref, wdn_ref, o_ref,
                hbuf, ubuf, abuf):
    i = pl.program_id(0)
    j = lax.rem(i, nts)
    keep_p = jnp.where(j == 0, 0.0, 1.0).astype(F32)
    keep_n = jnp.where(j == nts - 1, 0.0, 1.0).astype(F32)
    gn = gn_ref[...]

    def normed(x):
        ms = jnp.mean(x * x, axis=-1, keepdims=True)
        return x * lax.rsqrt(ms + EPS) * gn

    hbuf[0:FFN_HALO, :] = (normed(xp_ref[...]) * keep_p).astype(BF16)
    hbuf[FFN_HALO:FFN_HALO + TILE, :] = normed(x_ref[...]).astype(BF16)
    hbuf[FFN_HALO + TILE:, :] = (normed(xn_ref[...]) * keep_n).astype(BF16)

    pad = (FFN_K - 1) // 2
    chunks = _ffn_chunks()

    def up(ci):
        c0, w = chunks[ci]
        slot = ci % 2
        h = hbuf[...]
        ubuf[2 * slot, :, 0:w] = _dot(h, wup_ref[:, c0:c0 + w])
        ubuf[2 * slot + 1, :, 0:w] = _dot(h, wup_ref[:, D_FF + c0:D_FF + c0 + w])

    def conv(buf_idx, col, w):
        y = db_ref[:, col:col + w]
        for k in range(FFN_K):
            st = FFN_HALO - pad + k
            y = y + ubuf[buf_idx, st:st + TILE, 0:w] * dw_ref[k:k + 1, col:col + w]
        return y

    up(0)
    for ci, (c0, w) in enumerate(chunks):
        if ci + 1 < len(chunks):
            up(ci + 1)
        slot = ci % 2
        ya = conv(2 * slot, c0, w)
        yb = conv(2 * slot + 1, D_FF + c0, w)
        abuf[:, c0:c0 + w] = (ya * _sigmoid(ya) * yb).astype(BF16)

    act = abuf[...]
    for n0 in range(0, D_MODEL, MXU_DIM):
        cols = slice(n0, n0 + MXU_DIM)
        o_ref[:, cols] = x_ref[:, cols] + _dot(act, wdn_ref[:, cols])


def _ffn(x, gn, wup, dw, db, wdn, seq):
    t = x.shape[0]
    nts = seq // TILE
    hb = TILE // FFN_HALO
    row = lambda i: (i, 0)
    const = lambda i: (0, 0)
    prev = lambda i: (jnp.maximum(i * hb - 1, 0), 0)
    nxt = lambda i: (jnp.minimum((i + 1) * hb, t // FFN_HALO - 1), 0)
    single = pl.Buffered(1)
    return pl.pallas_call(
        functools.partial(_ffn_kernel, nts),
        grid=(t // TILE,),
        in_specs=[
            pl.BlockSpec((TILE, D_MODEL), row),
            pl.BlockSpec((FFN_HALO, D_MODEL), prev),
            pl.BlockSpec((FFN_HALO, D_MODEL), nxt),
            pl.BlockSpec((1, D_MODEL), const),
            pl.BlockSpec((D_MODEL, 2 * D_FF), const, pipeline_mode=single),
            pl.BlockSpec((FFN_K, 2 * D_FF), const),
            pl.BlockSpec((1, 2 * D_FF), const),
            pl.BlockSpec((D_FF, D_MODEL), const, pipeline_mode=single),
        ],
        out_specs=pl.BlockSpec((TILE, D_MODEL), row),
        out_shape=jax.ShapeDtypeStruct((t, D_MODEL), F32),
        scratch_shapes=[
            pltpu.VMEM((TILE + 2 * FFN_HALO, D_MODEL), BF16),
            pltpu.VMEM((4, TILE + 2 * FFN_HALO, FF_CHUNK), F32),
            pltpu.VMEM((TILE, D_FF), BF16),
        ],
        compiler_params=pltpu.CompilerParams(
            dimension_semantics=("arbitrary",), vmem_limit_bytes=VMEM_LIMIT),
        name="ffn",
    )(x, x, x, gn, wup, dw, db, wdn)


def _rope_tables(seq):
    half = ROT_DIM // 2
    inv = ROPE_THETA ** (-jnp.arange(half, dtype=F32) / half)
    ang = jnp.arange(seq, dtype=F32)[:, None] * inv[None, :]
    cos = jnp.cos(ang)
    sin = jnp.sin(ang)
    ones = jnp.ones((seq, HEAD_DIM - ROT_DIM), F32)
    zeros = jnp.zeros((seq, HEAD_DIM - ROT_DIM), F32)
    cos_h = jnp.concatenate([cos, cos, ones], axis=1)
    sin_h = jnp.concatenate([-sin, sin, zeros], axis=1)
    reps = LANES // HEAD_DIM
    return jnp.tile(cos_h, (1, reps)), jnp.tile(sin_h, (1, reps))


def _band_bias():
    qi = np.arange(WINDOW)[:, None]
    kj = np.arange(3 * WINDOW)[None, :] - WINDOW
    return jnp.asarray(np.where(np.abs(kj - qi) <= WINDOW, 0.0, NEG), F32)


def _head_mean_matrix():
    r = np.arange(2 * LANES)[:, None] % LANES
    c = np.arange(LANES)[None, :]
    return jnp.asarray(np.where(r // HEAD_DIM == c // HEAD_DIM, 1.0 / HEAD_DIM, 0.0), BF16)


def _prep_layer(norm_mix, w_in, q_norm, k_norm, sink, w_attn_out, conv_dw_w, conv_dw_b,
                conv_ln_g, conv_ln_b, w_conv_out, w_o, norm_ffn, w_up, ffn_dw_w, ffn_dw_b, w_down):
    o1 = Q_W
    o2 = o1 + N_KV_HEADS * HEAD_DIM
    o3 = o2 + N_KV_HEADS * HEAD_DIM
    k0, k1 = w_in[:, o1:o1 + HEAD_DIM], w_in[:, o1 + HEAD_DIM:o2]
    v0, v1 = w_in[:, o2:o2 + HEAD_DIM], w_in[:, o2 + HEAD_DIM:o3]
    w_cat = jnp.concatenate([w_in[:, :o1], k0, k0, k1, k1, v0, v1, v1, v0, w_in[:, o3:]], axis=1)
    return dict(
        gmix=norm_mix[None, :],
        w_in=w_cat.astype(BF16),
        gq=jnp.tile(q_norm, N_HEADS)[None, :],
        gk=jnp.tile(k_norm, 2 * N_KV_HEADS)[None, :],
        sink=sink,
        wa=w_attn_out.astype(BF16),
        dww8=jnp.repeat(conv_dw_w, SUBLANES, axis=0),
        dwb=conv_dw_b[None, :],
        lng=conv_ln_g[None, :],
        lnb=conv_ln_b[None, :],
        wc=w_conv_out.astype(BF16),
        wo=w_o.astype(BF16),
        gn=norm_ffn[None, :],
        wup=w_up.astype(BF16),
        dw=ffn_dw_w,
        db=ffn_dw_b[None, :],
        wdn=w_down.astype(BF16),
    )


def kernel(x_prompt, x_sample, norm_mix, w_in, q_norm, k_norm, sink, w_attn_out, conv_dw_w,
           conv_dw_b, conv_ln_g, conv_ln_b, w_conv_out, w_o, norm_ffn, w_up, ffn_dw_w, ffn_dw_b,
           w_down):
    depth = w_in.shape[0]
    stacked = (norm_mix, w_in, q_norm, k_norm, sink, w_attn_out, conv_dw_w, conv_dw_b, conv_ln_g,
               conv_ln_b, w_conv_out, w_o, norm_ffn, w_up, ffn_dw_w, ffn_dw_b, w_down)
    layers = [_prep_layer(*(a[l] for a in stacked)) for l in range(depth)]
    band = _band_bias()
    bm = _head_mean_matrix()

    def trunk(x):
        b, seq, d = x.shape
        assert d == D_MODEL and seq % TILE == 0
        cos_t, sin_t = _rope_tables(seq)
        y = x.reshape(b * seq, d)
        for p in layers:
            q, kk, vv, act, g = _inproj(y, p["gmix"], p["w_in"], p["gq"], p["gk"], cos_t, sin_t,
                                        bm, p["dww8"], p["dwb"], p["lng"], p["lnb"], seq)
            y = _mixer(y, q, kk, vv, act, g, p["sink"], band, p["wa"], p["wc"], p["wo"], seq)
            y = _ffn(y, p["gn"], p["wup"], p["dw"], p["db"], p["wdn"], seq)
        return y.reshape(b, seq, d)

    return (trunk(x_prompt), trunk(x_sample))
```
